```python
import math
import jax, jax.numpy as jnp
from jax import lax
import numpy as np

D_MODEL = 1024
BATCH = 32
SEQ = 2048
DEPTH = 1

DA_HEADS = 4
DA_HEAD_DIM = 64
DA_V_DIM = 2 * DA_HEAD_DIM
MLA_HEADS = 4
MLA_Q_RANK = 256
MLA_KV_RANK = 128
MLA_NOPE_DIM = 128
MLA_ROPE_DIM = 64
MLA_V_DIM = 128
ROPE_THETA = 10000.0
MIX_WIDTH = DA_HEADS * DA_V_DIM + MLA_HEADS * MLA_V_DIM
DA_QK_COLS = 2 * DA_HEADS * DA_HEAD_DIM
IN_SPLITS = [DA_QK_COLS, 2 * DA_QK_COLS, 3 * DA_QK_COLS,
             3 * DA_QK_COLS + MLA_Q_RANK,
             3 * DA_QK_COLS + MLA_Q_RANK + MLA_KV_RANK]
IN_COLS = 3 * DA_QK_COLS + MLA_Q_RANK + MLA_KV_RANK + MLA_ROPE_DIM
REL_BUCKETS = 32
REL_MAX_DIST = 128
N_EXPERTS = 32
TOP_K = 4
D_FF = D_MODEL
SWIGLU_LIMIT = 7.0
SWIGLU_ALPHA = 1.702
MOE_BLOCK = 512
Q_BLOCK = 128
NORM_EPS = 1e-5

kernel_name = 'hymba_diffattn_mla_moe_encoder'


def rmsnorm(x, g):
    xf = x.astype(jnp.float32)
    y = xf * lax.rsqrt(jnp.mean(xf * xf, axis=-1, keepdims=True) + NORM_EPS)
    return (y * g.astype(jnp.float32)).astype(x.dtype)


def t5_bucket(rel):
    half = REL_BUCKETS // 2
    max_exact = half // 2
    n = jnp.abs(rel)
    nf = jnp.maximum(n, 1).astype(jnp.float32)
    large = max_exact + (jnp.log(nf / max_exact) / math.log(REL_MAX_DIST / max_exact)
                         * (half - max_exact)).astype(jnp.int32)
    large = jnp.minimum(large, half - 1)
    return jnp.where(rel > 0, half, 0) + jnp.where(n < max_exact, n, large)


def rope_tables(s):
    inv = ROPE_THETA ** (-jnp.arange(0, MLA_ROPE_DIM, 2, dtype=jnp.float32) / MLA_ROPE_DIM)
    ang = jnp.arange(s, dtype=jnp.float32)[:, None] * inv[None, :]
    return jnp.cos(ang), jnp.sin(ang)


def apply_rope(x, cos, sin):
    x1, x2 = jnp.split(x, 2, axis=-1)
    cos = cos.astype(x.dtype)
    sin = sin.astype(x.dtype)
    return jnp.concatenate([x1 * cos - x2 * sin, x1 * sin + x2 * cos], axis=-1)


def sweep_query_blocks(block_fn, q):
    b, h, s, d = q.shape
    nq = s // Q_BLOCK
    qb = q.reshape(b, h, nq, Q_BLOCK, d).transpose(2, 0, 1, 3, 4)
    out = lax.map(lambda a: block_fn(a[0], a[1]), (qb, jnp.arange(nq) * Q_BLOCK))
    return out.transpose(1, 0, 3, 2, 4).reshape(b, s, out.shape[2], out.shape[4])


def diff_attention(q, k, v, lam, rel_bias_table, subln_g, lambda_init):
    b, s = q.shape[:2]
    qh = q.transpose(0, 2, 1, 3)
    kh = k.transpose(0, 2, 1, 3)
    vh = v.transpose(0, 2, 1, 3)
    kpos = jnp.arange(s)
    scale = DA_HEAD_DIM ** -0.5
    table = rel_bias_table.astype(jnp.float32)

    def block(qblk, q0):
        qpos = q0 + jnp.arange(Q_BLOCK)
        bias = table[t5_bucket(kpos[None, :] - qpos[:, None])].transpose(2, 0, 1)
        logits = jnp.einsum('bhqd,bhkd->bhqk', qblk, kh).astype(jnp.float32) * scale + bias
        p = jax.nn.softmax(logits, axis=-1).reshape(b, DA_HEADS, 2, Q_BLOCK, s)
        w = (p[:, :, 0] - lam * p[:, :, 1]).astype(vh.dtype)
        return jnp.einsum('bhqk,bhkd->bhqd', w, vh)

    o = sweep_query_blocks(block, qh)
    return rmsnorm(o, subln_g) * (1.0 - lambda_init)


def mla_attention(c_q, c_kv, k_rope, q_norm_g, w_uq, kv_norm_g, w_ukv, cos, sin):
    b, s = c_q.shape[:2]
    q = (rmsnorm(c_q, q_norm_g) @ w_uq).reshape(b, s, MLA_HEADS, MLA_NOPE_DIM + MLA_ROPE_DIM)
    q_nope, q_pe = q[..., :MLA_NOPE_DIM], q[..., MLA_NOPE_DIM:]
    q_pe = apply_rope(q_pe, cos[:, None, :], sin[:, None, :])
    kv = (rmsnorm(c_kv, kv_norm_g) @ w_ukv).reshape(b, s, MLA_HEADS, MLA_NOPE_DIM + MLA_V_DIM)
    k_nope, v = kv[..., :MLA_NOPE_DIM], kv[..., MLA_NOPE_DIM:]
    k_pe = apply_rope(k_rope, cos, sin)
    k_pe = jnp.broadcast_to(k_pe[:, :, None, :], (b, s, MLA_HEADS, MLA_ROPE_DIM))
    qh = jnp.concatenate([q_nope, q_pe], axis=-1).transpose(0, 2, 1, 3)
    kh = jnp.concatenate([k_nope, k_pe], axis=-1).transpose(0, 2, 1, 3)
    vh = v.transpose(0, 2, 1, 3)
    scale = (MLA_NOPE_DIM + MLA_ROPE_DIM) ** -0.5

    def block(qblk, q0):
        logits = jnp.einsum('bhqd,bhkd->bhqk', qblk, kh).astype(jnp.float32) * scale
        p = jax.nn.softmax(logits, axis=-1).astype(vh.dtype)
        return jnp.einsum('bhqk,bhkd->bhqd', p, vh)

    return sweep_query_blocks(block, qh)


def moe_ffn(h, w_router, b_router, w_gate_up, b_gate_up, w_down, b_down):
    b, s, d = h.shape
    t = b * s
    xf = h.reshape(t, d)
    logits = (xf @ w_router + b_router).astype(jnp.float32)
    top_val, top_idx = lax.top_k(logits, TOP_K)
    gates = jax.nn.softmax(top_val, axis=-1)
    n_assign = t * TOP_K
    flat_e = top_idx.reshape(-1)
    flat_t = jnp.arange(n_assign, dtype=jnp.int32) // TOP_K
    flat_g = gates.reshape(-1)
    order = jnp.argsort(flat_e, stable=True)
    sorted_e = flat_e[order]
    counts = jnp.bincount(flat_e, length=N_EXPERTS)
    padded = ((counts + MOE_BLOCK - 1) // MOE_BLOCK) * MOE_BLOCK
    group_start = jnp.cumsum(counts) - counts
    pad_end = jnp.cumsum(padded)
    pad_start = pad_end - padded
    dest = pad_start[sorted_e] + (jnp.arange(n_assign) - group_start[sorted_e])
    n_blocks = -(-n_assign // MOE_BLOCK) + N_EXPERTS
    n_rows = n_blocks * MOE_BLOCK
    row_tok = jnp.zeros((n_rows,), jnp.int32).at[dest].set(flat_t[order])
    row_gate = jnp.zeros((n_rows,), jnp.float32).at[dest].set(flat_g[order])
    block_e = jnp.minimum(jnp.searchsorted(pad_end, jnp.arange(n_blocks) * MOE_BLOCK, side='right'),
                          N_EXPERTS - 1)

    def expert_block(args):
        tok, g, e = args
        xb = xf[tok]
        gu = xb @ w_gate_up[e] + b_gate_up[e]
        gate, up = gu[:, :D_FF], gu[:, D_FF:]
        gate = jnp.minimum(gate, SWIGLU_LIMIT)
        up = jnp.clip(up, -SWIGLU_LIMIT, SWIGLU_LIMIT)
        act = (up + 1.0) * (gate * jax.nn.sigmoid(SWIGLU_ALPHA * gate))
        y = act @ w_down[e] + b_down[e]
        return y * g.astype(y.dtype)[:, None]

    ys = lax.map(expert_block, (row_tok.reshape(n_blocks, MOE_BLOCK),
                                row_gate.reshape(n_blocks, MOE_BLOCK), block_e))
    out = jnp.zeros((t, d), h.dtype).at[row_tok].add(ys.reshape(n_rows, d).astype(h.dtype))
    return out.reshape(b, s, d)


def setup_inputs(seed: int = 0) -> dict:
    key = jax.random.key(seed)
    ks = jax.random.split(key, 24)
    f32 = jnp.float32
    L = DEPTH

    def nrm(k, shape, scale):
        return jax.random.normal(k, shape, f32) * scale

    return {
        'x': nrm(ks[0], (BATCH, SEQ, D_MODEL), 1.0),
        'rel_bias_table': nrm(ks[1], (REL_BUCKETS, 2 * DA_HEADS), 0.2),
        'attn_norm_g': 1.0 + nrm(ks[2], (L, D_MODEL), 0.01),
        'w_in': nrm(ks[3], (L, D_MODEL, IN_COLS), D_MODEL ** -0.5),
        'lam_q1': nrm(ks[4], (L, DA_HEAD_DIM), 0.1),
        'lam_k1': nrm(ks[5], (L, DA_HEAD_DIM), 0.1),
        'lam_q2': nrm(ks[6], (L, DA_HEAD_DIM), 0.1),
        'lam_k2': nrm(ks[7], (L, DA_HEAD_DIM), 0.1),
        'subln_g': 1.0 + nrm(ks[8], (L, DA_V_DIM), 0.01),
        'q_norm_g': 1.0 + nrm(ks[9], (L, MLA_Q_RANK), 0.01),
        'w_uq': nrm(ks[10], (L, MLA_Q_RANK, MLA_HEADS * (MLA_NOPE_DIM + MLA_ROPE_DIM)), MLA_Q_RANK ** -0.5),
        'kv_norm_g': 1.0 + nrm(ks[11], (L, MLA_KV_RANK), 0.01),
        'w_ukv': nrm(ks[12], (L, MLA_KV_RANK, MLA_HEADS * (MLA_NOPE_DIM + MLA_V_DIM)), MLA_KV_RANK ** -0.5),
        'w_o': nrm(ks[13], (L, MIX_WIDTH, D_MODEL), MIX_WIDTH ** -0.5),
        'ffn_norm_g': 1.0 + nrm(ks[14], (L, D_MODEL), 0.01),
        'w_router': nrm(ks[15], (L, D_MODEL, N_EXPERTS), D_MODEL ** -0.5),
        'b_router': nrm(ks[16], (L, N_EXPERTS), 0.01),
        'w_gate_up': nrm(ks[17], (L, N_EXPERTS, D_MODEL, 2 * D_FF), D_MODEL ** -0.5),
        'b_gate_up': nrm(ks[18], (L, N_EXPERTS, 2 * D_FF), 0.01),
        'w_down': nrm(ks[19], (L, N_EXPERTS, D_FF, D_MODEL), D_FF ** -0.5),
        'b_down': nrm(ks[20], (L, N_EXPERTS, D_MODEL), 0.01),
        'final_norm_g': 1.0 + nrm(ks[21], (D_MODEL,), 0.01),
    }


def reference(x, rel_bias_table, attn_norm_g, w_in, lam_q1, lam_k1, lam_q2, lam_k2, subln_g,
              q_norm_g, w_uq, kv_norm_g, w_ukv, w_o, ffn_norm_g, w_router, b_router,
              w_gate_up, b_gate_up, w_down, b_down, final_norm_g):
    b, s, _ = x.shape
    cos, sin = rope_tables(s)
    for l in range(DEPTH):
        lambda_init = 0.8 - 0.6 * math.exp(-0.3 * l)
        n = rmsnorm(x, attn_norm_g[l])
        proj = n @ w_in[l]
        qa, ka, va, c_q, c_kv, k_rope = jnp.split(proj, IN_SPLITS, axis=-1)
        lam = (jnp.exp(jnp.sum(lam_q1[l].astype(jnp.float32) * lam_k1[l].astype(jnp.float32)))
               - jnp.exp(jnp.sum(lam_q2[l].astype(jnp.float32) * lam_k2[l].astype(jnp.float32)))
               + lambda_init)
        oa = diff_attention(qa.reshape(b, s, 2 * DA_HEADS, DA_HEAD_DIM),
                            ka.reshape(b, s, 2 * DA_HEADS, DA_HEAD_DIM),
                            va.reshape(b, s, DA_HEADS, DA_V_DIM),
                            lam, rel_bias_table, subln_g[l], lambda_init)
        ob = mla_attention(c_q, c_kv, k_rope, q_norm_g[l], w_uq[l], kv_norm_g[l], w_ukv[l], cos, sin)
        mixed = jnp.concatenate([oa.reshape(b, s, -1), ob.reshape(b, s, -1)], axis=-1)
        x = x + mixed @ w_o[l]
        x = x + moe_ffn(rmsnorm(x, ffn_norm_g[l]), w_router[l], b_router[l],
                        w_gate_up[l], b_gate_up[l], w_down[l], b_down[l])
    return rmsnorm(x, final_norm_g)
```

```python
import functools
import math

import jax
import jax.numpy as jnp
from jax import lax
from jax.experimental import pallas as pl
from jax.experimental.pallas import tpu as pltpu

DA_HEADS = 4
DA_HEAD_DIM = 64
DA_V_DIM = 2 * DA_HEAD_DIM
MLA_HEADS = 4
MLA_Q_RANK = 256
MLA_KV_RANK = 128
MLA_NOPE_DIM = 128
MLA_ROPE_DIM = 64
MLA_V_DIM = 128
ROPE_THETA = 10000.0
DA_QK_COLS = 2 * DA_HEADS * DA_HEAD_DIM
REL_BUCKETS = 32
N_EXPERTS = 32
TOP_K = 4
SWIGLU_LIMIT = 7.0
SWIGLU_ALPHA = 1.702
NORM_EPS = 1e-5

LANES = 128
VMEM_LIMIT = 56 * 1024 * 1024

PROJ_ROWS = 512
Q_ROWS = 256
MOE_ROWS = 512
DISPATCH_ROWS = 512
COMBINE_ROWS = 256
META_COLS = LANES

BF16 = jnp.bfloat16
F32 = jnp.float32


def _cparams(sem):
    return pltpu.CompilerParams(dimension_semantics=sem, vmem_limit_bytes=VMEM_LIMIT)


def _rms(v, g):
    return v * lax.rsqrt(jnp.mean(v * v, axis=-1, keepdims=True) + NORM_EPS) * g


def _bias_kernel(table_ref, lq1_ref, lk1_ref, lq2_ref, lk2_ref, tiles_ref, lam_ref, *, lambda_init):
    r = lax.broadcasted_iota(jnp.int32, (LANES, LANES), 0)
    k = lax.broadcasted_iota(jnp.int32, (LANES, LANES), 1)
    half = REL_BUCKETS // 2
    max_exact = half // 2
    for d in range(3):
        rel = (d - 1) * LANES + k - r
        n = jnp.abs(rel)
        n2 = n * n
        large = jnp.full_like(n, max_exact)
        for j in range(1, half - max_exact):
            large = large + (n2 >= (1 << (j + 6))).astype(jnp.int32)
        bucket = jnp.where(rel > 0, half, 0) + jnp.where(n < max_exact, n, large)
        for m in range(2 * DA_HEADS):
            acc = jnp.zeros((LANES, LANES), F32)
            for b in range(REL_BUCKETS):
                acc = jnp.where(bucket == b, table_ref[b, m], acc)
            tiles_ref[m, d] = acc
    s1 = jnp.sum(lq1_ref[...] * lk1_ref[...], axis=-1, keepdims=True)
    s2 = jnp.sum(lq2_ref[...] * lk2_ref[...], axis=-1, keepdims=True)
    lam = jnp.exp(s1) - jnp.exp(s2) + lambda_init
    lam_ref[...] = jnp.broadcast_to(lam, lam_ref.shape)


def _bias_tiles(table, lq1, lk1, lq2, lk2, lambda_init):
    vm = pl.BlockSpec(memory_space=pltpu.VMEM)
    return pl.pallas_call(
        functools.partial(_bias_kernel, lambda_init=lambda_init),
        out_shape=(jax.ShapeDtypeStruct((2 * DA_HEADS, 3, LANES, LANES), F32),
                   jax.ShapeDtypeStruct((1, LANES), F32)),
        in_specs=[pl.BlockSpec(memory_space=pltpu.SMEM), vm, vm, vm, vm],
        out_specs=(vm, vm),
        name="bias_tiles",
    )(table.astype(F32), lq1.reshape(1, -1).astype(F32), lk1.reshape(1, -1).astype(F32),
      lq2.reshape(1, -1).astype(F32), lk2.reshape(1, -1).astype(F32))


N_A = 3 * DA_QK_COLS
C_CQ = N_A
C_CKV = C_CQ + MLA_Q_RANK
C_KRA = C_CKV + MLA_KV_RANK
C_KRB = C_KRA + LANES
N_ALL = C_KRB + LANES
QC_W = 2 * LANES
KVM_W = 2 * MLA_HEADS * LANES + LANES


def _proj_kernel(x_ref, g_ref, w_ref, gq_ref, wq_ref, gkv_ref, wkv_ref,
                 cq_ref, sq_ref, ck_ref, sk_ref, pa_ref, qc_ref, kvm_ref):
    x = x_ref[...]
    n = _rms(x, g_ref[...]).astype(BF16)
    res = jnp.dot(n, w_ref[...], preferred_element_type=F32)
    pa_ref[:, :DA_QK_COLS] = (res[:, :DA_QK_COLS] * (DA_HEAD_DIM ** -0.5)).astype(BF16)
    pa_ref[:, DA_QK_COLS:] = res[:, DA_QK_COLS:N_A].astype(BF16)
    cqn = _rms(res[:, C_CQ:C_CKV], gq_ref[...]).astype(BF16)
    qab = jnp.dot(cqn, wq_ref[...], preferred_element_type=F32)
    cq = cq_ref[...]
    sq = sq_ref[...]
    hw = MLA_HEADS * QC_W
    for h in range(MLA_HEADS):
        a = qab[:, h * QC_W:(h + 1) * QC_W]
        b = qab[:, hw + h * QC_W: hw + (h + 1) * QC_W]
        qc_ref[:, h * QC_W:(h + 1) * QC_W] = (a * cq + b * sq).astype(BF16)
    ckvn = _rms(res[:, C_CKV:C_KRA], gkv_ref[...]).astype(BF16)
    kv = jnp.dot(ckvn, wkv_ref[...], preferred_element_type=F32)
    kvm_ref[:, :2 * MLA_HEADS * LANES] = kv.astype(BF16)
    kpe = res[:, C_KRA:C_KRB] * ck_ref[...] + res[:, C_KRB:N_ALL] * sk_ref[...]
    kvm_ref[:, 2 * MLA_HEADS * LANES:] = kpe.astype(BF16)


def _swap_halves(w):
    h = w.shape[-1] // 2
    return jnp.concatenate([w[..., h:], w[..., :h]], axis=-1)


def _input_projection(x2d, seq, g, w_in, gq, w_uq, gkv, w_ukv):
    t, d = x2d.shape
    tm = min(PROJ_ROWS, seq)
    assert seq % tm == 0 and t % tm == 0
    r = MLA_ROPE_DIM
    w_kr = w_in[:, C_KRA:]
    zpad = jnp.zeros((d, LANES - r), w_in.dtype)
    w_all = jnp.concatenate(
        [w_in[:, :C_KRA], w_kr, zpad, _swap_halves(w_kr), zpad],
        axis=1).astype(BF16)
    wq = w_uq.reshape(MLA_Q_RANK, MLA_HEADS, MLA_NOPE_DIM + r)
    zq = jnp.zeros((MLA_Q_RANK, MLA_HEADS, QC_W - MLA_NOPE_DIM - r), w_uq.dtype)
    wq_a = jnp.concatenate([wq, zq], axis=-1).reshape(MLA_Q_RANK, MLA_HEADS * QC_W)
    wq_b = jnp.concatenate([jnp.zeros_like(wq[..., :MLA_NOPE_DIM]), _swap_halves(wq[..., MLA_NOPE_DIM:]), zq],
                           axis=-1).reshape(MLA_Q_RANK, MLA_HEADS * QC_W)
    wq_ab = jnp.concatenate([wq_a, wq_b], axis=1).astype(BF16)
    wkv = w_ukv.reshape(MLA_KV_RANK, MLA_HEADS, MLA_NOPE_DIM + MLA_V_DIM)
    wkv2 = jnp.concatenate([wkv[..., :MLA_NOPE_DIM].reshape(MLA_KV_RANK, -1),
                            wkv[..., MLA_NOPE_DIM:].reshape(MLA_KV_RANK, -1)], axis=1).astype(BF16)
    inv = ROPE_THETA ** (-jnp.arange(0, r, 2, dtype=F32) / r)
    ang = jnp.arange(seq, dtype=F32)[:, None] * inv[None, :]
    cos, sin = jnp.cos(ang), jnp.sin(ang)
    cc = jnp.concatenate([cos, cos], axis=1)
    ss = jnp.concatenate([-sin, sin], axis=1)
    qscale = (MLA_NOPE_DIM + r) ** -0.5
    zq_t = jnp.zeros((seq, QC_W - MLA_NOPE_DIM - r), F32)
    cq_t = jnp.concatenate([jnp.ones((seq, MLA_NOPE_DIM), F32), cc, zq_t], axis=1) * qscale
    sq_t = jnp.concatenate([jnp.zeros((seq, MLA_NOPE_DIM), F32), ss, zq_t], axis=1) * qscale
    zk_t = jnp.zeros((seq, LANES - r), F32)
    ck_t = jnp.concatenate([cc, zk_t], axis=1)
    sk_t = jnp.concatenate([ss, zk_t], axis=1)

    spt = seq // tm
    full = lambda a: pl.BlockSpec(a.shape, lambda i: (0,) * a.ndim)
    tab = lambda w: pl.BlockSpec((tm, w), lambda i: (i % spt, 0))
    g2, gq2, gkv2 = g.reshape(1, -1).astype(F32), gq.reshape(1, -1).astype(F32), gkv.reshape(1, -1).astype(F32)
    return pl.pallas_call(
        _proj_kernel,
        grid=(t // tm,),
        out_shape=(jax.ShapeDtypeStruct((t, N_A), BF16),
                   jax.ShapeDtypeStruct((t, MLA_HEADS * QC_W), BF16),
                   jax.ShapeDtypeStruct((t, KVM_W), BF16)),
        in_specs=[pl.BlockSpec((tm, d), lambda i: (i, 0)), full(g2), full(w_all), full(gq2), full(wq_ab),
                  full(gkv2), full(wkv2), tab(QC_W), tab(QC_W), tab(LANES), tab(LANES)],
        out_specs=(pl.BlockSpec((tm, N_A), lambda i: (i, 0)),
                   pl.BlockSpec((tm, MLA_HEADS * QC_W), lambda i: (i, 0)),
                   pl.BlockSpec((tm, KVM_W), lambda i: (i, 0))),
        compiler_params=_cparams(("arbitrary",)),
        name="input_projection",
    )(x2d, g2, w_all, gq2, wq_ab, gkv2, wkv2, cq_t, sq_t, ck_t, sk_t)


def _softmax_parts(s):
    m = jnp.max(s, axis=-1, keepdims=True)
    e = jnp.exp(s - m)
    return e, 1.0 / jnp.sum(e, axis=-1, keepdims=True)


def _diff_attn_kernel(table_ref, lam_ref, q_ref, k_ref, v_ref, tiles_ref, g_ref, o_ref, bias_ref,
                      *, seq, tq, out_scale):
    h = pl.program_id(0)
    qi = pl.program_id(1)
    b = pl.program_id(2)
    half = REL_BUCKETS // 2

    @pl.when(b == 0)
    def _build_bias():
        for m in range(2):
            c_neg = table_ref[half - 1, 2 * h + m]
            c_pos = table_ref[REL_BUCKETS - 1, 2 * h + m]
            for rb in range(tq // LANES):
                for kb in range(seq // LANES):
                    d = kb - (qi * (tq // LANES) + rb)
                    near = tiles_ref[m, jnp.clip(d + 1, 0, 2)]
                    far = jnp.where(d < 0, c_neg, c_pos)
                    tile = jnp.where(jnp.abs(d) <= 1, near, far)
                    bias_ref[m, rb * LANES:(rb + 1) * LANES, kb * LANES:(kb + 1) * LANES] = tile

    q = q_ref[...]
    lane = lax.broadcasted_iota(jnp.int32, q.shape, 1)
    zero = jnp.zeros_like(q)
    k = k_ref[...]
    nt = (((1,), (1,)), ((), ()))
    s1 = lax.dot_general(jnp.where(lane < DA_HEAD_DIM, q, zero), k, nt, preferred_element_type=F32) + bias_ref[0]
    s2 = lax.dot_general(jnp.where(lane >= DA_HEAD_DIM, q, zero), k, nt, preferred_element_type=F32) + bias_ref[1]
    e1, r1 = _softmax_parts(s1)
    e2, r2 = _softmax_parts(s2)
    lam = lam_ref[0, 0]
    w = (e1 * r1 - e2 * (r2 * lam)).astype(BF16)
    o = jnp.dot(w, v_ref[...], preferred_element_type=F32)
    o_ref[...] = (_rms(o, g_ref[...]) * out_scale).astype(o_ref.dtype)


def _diff_attention(pa, table, lam, tiles, subln_g, batch, seq, lambda_init):
    tq = min(Q_ROWS, seq)
    nq = seq // tq
    hcols = DA_QK_COLS // LANES
    g2 = subln_g.reshape(1, -1).astype(F32)
    kern = functools.partial(_diff_attn_kernel, seq=seq, tq=tq, out_scale=1.0 - lambda_init)
    return pl.pallas_call(
        kern,
        grid=(DA_HEADS, nq, batch),
        out_shape=jax.ShapeDtypeStruct((batch * seq, DA_HEADS * DA_V_DIM), BF16),
        in_specs=[pl.BlockSpec(memory_space=pltpu.SMEM),
                  pl.BlockSpec(memory_space=pltpu.SMEM),
                  pl.BlockSpec((tq, LANES), lambda h, qi, b: (b * nq + qi, h)),
                  pl.BlockSpec((seq, LANES), lambda h, qi, b: (b, hcols + h)),
                  pl.BlockSpec((seq, LANES), lambda h, qi, b: (b, 2 * hcols + h)),
                  pl.BlockSpec((2, 3, LANES, LANES), lambda h, qi, b: (h, 0, 0, 0)),
                  pl.BlockSpec((1, DA_V_DIM), lambda h, qi, b: (0, 0))],
        out_specs=pl.BlockSpec((tq, LANES), lambda h, qi, b: (b * nq + qi, h)),
        scratch_shapes=[pltpu.VMEM((2, tq, seq), F32)],
        compiler_params=_cparams(("arbitrary", "arbitrary", "arbitrary")),
        name="diff_attention",
    )(table.astype(F32), lam, pa, pa, pa, tiles, g2)


def _mla_kernel(q_ref, kn_ref, kpe_ref, v_ref, o_ref, kc_ref):
    @pl.when(pl.program_id(2) == 0)
    def _assemble_keys():
        kc_ref[:, :LANES] = kn_ref[...]
        kc_ref[:, LANES:] = kpe_ref[...]

    nt = (((1,), (1,)), ((), ()))
    s = lax.dot_general(q_ref[...], kc_ref[...], nt, preferred_element_type=F32)
    e, r = _softmax_parts(s)
    o = jnp.dot(e.astype(BF16), v_ref[...], preferred_element_type=F32) * r
    o_ref[...] = o.astype(o_ref.dtype)


def _mla_attention(qc, kvm, batch, seq):
    tq = min(Q_ROWS, seq)
    nq = seq // tq
    return pl.pallas_call(
        _mla_kernel,
        grid=(batch, MLA_HEADS, nq),
        out_shape=jax.ShapeDtypeStruct((batch * seq, MLA_HEADS * MLA_V_DIM), BF16),
        in_specs=[pl.BlockSpec((tq, QC_W), lambda b, h, qi: (b * nq + qi, h)),
                  pl.BlockSpec((seq, LANES), lambda b, h, qi: (b, h)),
                  pl.BlockSpec((seq, LANES), lambda b, h, qi: (b, 2 * MLA_HEADS)),
                  pl.BlockSpec((seq, LANES), lambda b, h, qi: (b, MLA_HEADS + h))],
        out_specs=pl.BlockSpec((tq, LANES), lambda b, h, qi: (b * nq + qi, h)),
        scratch_shapes=[pltpu.VMEM((seq, QC_W), BF16)],
        compiler_params=_cparams(("arbitrary", "arbitrary", "arbitrary")),
        name="mla_attention",
    )(qc, kvm, kvm, kvm)


def _router_kernel(x_ref, oa_ref, ob_ref, wo_ref, g_ref, wr_ref, br_ref,
                   x1_ref, h_ref, meta_ref, cnt_ref, carry_ref):
    @pl.when(pl.program_id(0) == 0)
    def _init():
        carry_ref[...] = jnp.zeros_like(carry_ref)

    na = oa_ref.shape[1]
    x1 = (x_ref[...] + jnp.dot(oa_ref[...], wo_ref[:na, :], preferred_element_type=F32)
          + jnp.dot(ob_ref[...], wo_ref[na:, :], preferred_element_type=F32))
    x1_ref[...] = x1
    hn = _rms(x1, g_ref[...])
    h_ref[...] = hn
    logits = jnp.dot(hn, wr_ref[...], preferred_element_type=F32,
                     precision=lax.Precision.HIGHEST) + br_ref[...]
    tm, ne = logits.shape
    eidx = lax.broadcasted_iota(jnp.int32, (tm, ne), 1)
    vals, idxs = [], []
    l = logits
    for _ in range(TOP_K):
        m = jnp.max(l, axis=-1, keepdims=True)
        idx = jnp.min(jnp.where(l == m, eidx, ne), axis=-1, keepdims=True)
        vals.append(m)
        idxs.append(idx)
        l = jnp.where(eidx == idx, -jnp.inf, l)
    ex = [jnp.exp(v - vals[0]) for v in vals]
    denom = ex[0]
    for e in ex[1:]:
        denom = denom + e
    gates = [e / denom for e in ex]
    onehot = jnp.zeros((tm, ne), F32)
    for idx in idxs:
        onehot = onehot + (eidx == idx).astype(F32)
    row = lax.broadcasted_iota(jnp.int32, (tm, tm), 0)
    col = lax.broadcasted_iota(jnp.int32, (tm, tm), 1)
    tri = (col < row).astype(BF16)
    before = jnp.dot(tri, onehot.astype(BF16), preferred_element_type=F32) + carry_ref[...]
    lane = lax.broadcasted_iota(jnp.int32, (tm, META_COLS), 1)
    meta = jnp.zeros((tm, META_COLS), F32)
    for kk in range(TOP_K):
        rank = jnp.sum(jnp.where(eidx == idxs[kk], before, 0.0), axis=-1, keepdims=True)
        meta = jnp.where(lane == kk, idxs[kk].astype(F32), meta)
        meta = jnp.where(lane == TOP_K + kk, gates[kk], meta)
        meta = jnp.where(lane == 2 * TOP_K + kk, rank, meta)
    meta_ref[...] = meta
    carry_ref[...] = carry_ref[...] + jnp.sum(onehot, axis=0, keepdims=True)
    cnt_ref[...] = carry_ref[...]


def _out_proj_router(x2d, oa, ob, w_o, g, w_router, b_router):
    t, d = x2d.shape
    tm = min(PROJ_ROWS, t)
    assert t % tm == 0
    ne = w_router.shape[1]
    g2 = g.reshape(1, -1).astype(F32)
    br = b_router.reshape(1, -1).astype(F32)
    wo = w_o.astype(BF16)
    full = lambda a: pl.BlockSpec(a.shape, lambda i: (0,) * a.ndim)
    rows = lambda w: pl.BlockSpec((tm, w), lambda i: (i, 0))
    return pl.pallas_call(
        _router_kernel,
        grid=(t // tm,),
        out_shape=(jax.ShapeDtypeStruct((t, d), F32), jax.ShapeDtypeStruct((t, d), F32),
                   jax.ShapeDtypeStruct((t, META_COLS), F32), jax.ShapeDtypeStruct((1, ne), F32)),
        in_specs=[rows(d), rows(oa.shape[1]), rows(ob.shape[1]), full(wo), full(g2),
                  full(w_router), full(br)],
        out_specs=(rows(d), rows(d), rows(META_COLS), pl.BlockSpec((1, ne), lambda i: (0, 0))),
        scratch_shapes=[pltpu.VMEM((1, ne), F32)],
        compiler_params=_cparams(("arbitrary",)),
        name="out_proj_router",
    )(x2d, oa, ob, wo, g2, w_router.astype(F32), br)


def _row_copy(src_hbm, dst_hbm, src_row, dst_row, sem):
    return pltpu.make_async_copy(src_hbm.at[pl.ds(src_row, 1)], dst_hbm.at[pl.ds(dst_row, 1)], sem)


def _dispatch_kernel(dest_ref, h_hbm, xs_hbm, sem, *, tb):
    base = pl.program_id(0) * tb

    def issue(t, carry):
        for kk in range(TOP_K):
            _row_copy(h_hbm, xs_hbm, base + t, dest_ref[0, 0, t * TOP_K + kk], sem).start()
        return carry

    lax.fori_loop(0, tb, issue, 0)
    pltpu.make_async_copy(h_hbm.at[pl.ds(0, tb * TOP_K)], xs_hbm.at[pl.ds(0, tb * TOP_K)], sem).wait()


def _dispatch(h, dest):
    t, d = h.shape
    tb = min(DISPATCH_ROWS, t)
    assert t % tb == 0
    dest3 = dest.reshape(t // tb, 1, tb * TOP_K)
    return pl.pallas_call(
        functools.partial(_dispatch_kernel, tb=tb),
        grid=(t // tb,),
        out_shape=jax.ShapeDtypeStruct((t * TOP_K, d), h.dtype),
        in_specs=[pl.BlockSpec((1, 1, tb * TOP_K), lambda i: (i, 0, 0), memory_space=pltpu.SMEM),
                  pl.BlockSpec(memory_space=pl.ANY)],
        out_specs=pl.BlockSpec(memory_space=pl.ANY),
        scratch_shapes=[pltpu.SemaphoreType.DMA(())],
        compiler_params=_cparams(("arbitrary",)),
        name="moe_dispatch",
    )(dest3, h)


def _expert_kernel(grp_ref, tile_ref, start_ref, end_ref, nitems_ref,
                   xs_ref, wgu_ref, bgu_ref, wd_ref, bd_ref, ys_ref, *, tm, d_ff):
    w = pl.program_id(0)

    @pl.when(w < nitems_ref[0])
    def _compute():
        e = grp_ref[w]
        tile = tile_ref[w]
        x = xs_ref[...].astype(BF16)
        gu = jnp.dot(x, wgu_ref[0], preferred_element_type=F32) + bgu_ref[0]
        gate = jnp.minimum(gu[:, :d_ff], SWIGLU_LIMIT)
        up = jnp.clip(gu[:, d_ff:], -SWIGLU_LIMIT, SWIGLU_LIMIT)
        act = (up + 1.0) * (gate * jax.nn.sigmoid(SWIGLU_ALPHA * gate))
        y = jnp.dot(act.astype(BF16), wd_ref[0], preferred_element_type=F32) + bd_ref[0]
        rows = tile * tm + lax.broadcasted_iota(jnp.int32, (tm, 1), 0)
        mine = (rows >= start_ref[e]) & (rows < end_ref[e])
        first = jnp.logical_or(w == 0, tile_ref[jnp.maximum(w - 1, 0)] != tile)
        prev = jnp.where(first, 0.0, ys_ref[...])
        ys_ref[...] = jnp.where(mine, y, prev)


def _expert_ffn(xs, counts, w_gate_up, b_gate_up, w_down, b_down):
    n, d = xs.shape
    ne, _, two_ff = w_gate_up.shape
    d_ff = two_ff // 2
    tm = min(MOE_ROWS, n)
    assert n % tm == 0
    n_tiles = n // tm
    n_work = n_tiles + ne - 1
    ends = jnp.cumsum(counts)
    starts = ends - counts
    first_tile = starts // tm
    tiles_per = jnp.where(counts > 0, (ends + tm - 1) // tm - first_tile, 0)
    item_end = jnp.cumsum(tiles_per)
    n_items = item_end[-1]
    item = jnp.arange(n_work, dtype=jnp.int32)
    grp = jnp.minimum(jnp.searchsorted(item_end, item, side='right'), ne - 1).astype(jnp.int32)
    tile = (first_tile[grp] + item - (item_end - tiles_per)[grp]).astype(jnp.int32)
    last = jnp.maximum(n_items - 1, 0)
    valid = item < n_items
    grp = jnp.where(valid, grp, grp[last])
    tile = jnp.where(valid, tile, tile[last])

    wgu = w_gate_up.astype(BF16)
    wd = w_down.astype(BF16)
    bgu = b_gate_up.reshape(ne, 1, two_ff).astype(F32)
    bd = b_down.reshape(ne, 1, d).astype(F32)
    grid_spec = pltpu.PrefetchScalarGridSpec(
        num_scalar_prefetch=5,
        grid=(n_work,),
        in_specs=[pl.BlockSpec((tm, d), lambda w, g, tl, s, e, ni: (tl[w], 0)),
                  pl.BlockSpec((1, d, two_ff), lambda w, g, tl, s, e, ni: (g[w], 0, 0)),
                  pl.BlockSpec((1, 1, two_ff), lambda w, g, tl, s, e, ni: (g[w], 0, 0)),
                  pl.BlockSpec((1, d_ff, d), lambda w, g, tl, s, e, ni: (g[w], 0, 0)),
                  pl.BlockSpec((1, 1, d), lambda w, g, tl, s, e, ni: (g[w], 0, 0))],
        out_specs=pl.BlockSpec((tm, d), lambda w, g, tl, s, e, ni: (tl[w], 0)),
    )
    return pl.pallas_call(
        functools.partial(_expert_kernel, tm=tm, d_ff=d_ff),
        grid_spec=grid_spec,
        out_shape=jax.ShapeDtypeStruct((n, d), F32),
        compiler_params=_cparams(("arbitrary",)),
        name="expert_ffn",
    )(grp, tile, starts.astype(jnp.int32), ends.astype(jnp.int32),
      n_items.reshape(1).astype(jnp.int32), xs, wgu, bgu, wd, bd)


def _combine_kernel(dest_ref, x1_ref, meta_ref, g_ref, ys_hbm, o_ref, buf_ref, sem, *, tb, final_norm):
    def issue(t, carry):
        for kk in range(TOP_K):
            pltpu.make_async_copy(ys_hbm.at[pl.ds(dest_ref[0, 0, t * TOP_K + kk], 1)],
                                  buf_ref.at[kk, pl.ds(t, 1)], sem).start()
        return carry

    lax.fori_loop(0, tb, issue, 0)
    for kk in range(TOP_K):
        pltpu.make_async_copy(ys_hbm.at[pl.ds(0, tb)], buf_ref.at[kk], sem).wait()
    meta = meta_ref[...]
    acc = x1_ref[...]
    for kk in range(TOP_K):
        acc = acc + meta[:, TOP_K + kk:TOP_K + kk + 1] * buf_ref[kk]
    if final_norm:
        acc = _rms(acc, g_ref[...])
    o_ref[...] = acc


def _combine(x1, meta, dest, ys, g, final_norm):
    t, d = x1.shape
    tb = min(COMBINE_ROWS, t)
    assert t % tb == 0
    dest3 = dest.reshape(t // tb, 1, tb * TOP_K)
    g2 = g.reshape(1, -1).astype(F32)
    return pl.pallas_call(
        functools.partial(_combine_kernel, tb=tb, final_norm=final_norm),
        grid=(t // tb,),
        out_shape=jax.ShapeDtypeStruct((t, d), F32),
        in_specs=[pl.BlockSpec((1, 1, tb * TOP_K), lambda i: (i, 0, 0), memory_space=pltpu.SMEM),
                  pl.BlockSpec((tb, d), lambda i: (i, 0)),
                  pl.BlockSpec((tb, META_COLS), lambda i: (i, 0)),
                  pl.BlockSpec((1, d), lambda i: (0, 0)),
                  pl.BlockSpec(memory_space=pl.ANY)],
        out_specs=pl.BlockSpec((tb, d), lambda i: (i, 0)),
        scratch_shapes=[pltpu.VMEM((TOP_K, tb, d), F32), pltpu.SemaphoreType.DMA(())],
        compiler_params=_cparams(("arbitrary",)),
        name="moe_combine",
    )(dest3, x1, meta, g2, ys)


def kernel(x, rel_bias_table, attn_norm_g, w_in, lam_q1, lam_k1, lam_q2, lam_k2, subln_g, q_norm_g, w_uq, kv_norm_g, w_ukv, w_o, ffn_norm_g, w_router, b_router, w_gate_up, b_gate_up, w_down, b_down, final_norm_g):
    batch, seq, d = x.shape
    depth = w_in.shape[0]
    x2d = x.reshape(batch * seq, d)
    for l in range(depth):
        lambda_init = 0.8 - 0.6 * math.exp(-0.3 * l)
        tiles, lam = _bias_tiles(rel_bias_table, lam_q1[l], lam_k1[l], lam_q2[l], lam_k2[l], lambda_init)
        pa, qc, kvm = _input_projection(x2d, seq, attn_norm_g[l], w_in[l], q_norm_g[l], w_uq[l],
                                        kv_norm_g[l], w_ukv[l])
        oa = _diff_attention(pa, rel_bias_table, lam, tiles, subln_g[l], batch, seq, lambda_init)
        ob = _mla_attention(qc, kvm, batch, seq)
        x1, h, meta, counts = _out_proj_router(x2d, oa, ob, w_o[l], ffn_norm_g[l], w_router[l], b_router[l])
        counts = counts[0].astype(jnp.int32)
        top_idx = meta[:, :TOP_K].astype(jnp.int32)
        rank = meta[:, 2 * TOP_K:3 * TOP_K].astype(jnp.int32)
        starts = jnp.cumsum(counts) - counts
        dest = (starts[top_idx] + rank).reshape(-1)
        xs = _dispatch(h, dest)
        ys = _expert_ffn(xs, counts, w_gate_up[l], b_gate_up[l], w_down[l], b_down[l])
        x2d = _combine(x1, meta, dest, ys, final_norm_g, final_norm=(l == depth - 1))
    return x2d.reshape(batch, seq, d)
```
